```python
import math
import jax, jax.numpy as jnp
from jax import lax
import numpy as np

D_MODEL = 1024
BATCH = 8
SEQ = 4096
DEPTH = 2

CTX_LEN = 256
GRID_W = 64
ROPE_THETA = 10000.0
EPS = 1e-6
Q_BLOCK = 128
N_MOD = 9

D_FF = 2816

CHUNK = 128
A_GROUPS = 8
A_WIDTH = 1024
A_GROUP_W = A_WIDTH // A_GROUPS

B_HEADS = 8
B_HEAD_DIM = 64
B_WIDTH = B_HEADS * 2 * B_HEAD_DIM

C_HEADS = 16
C_NOPE = 64
C_ROPE = 32
C_VDIM = 64
C_Q_RANK = 512
C_KV_RANK = 256
C_WIDTH = C_HEADS * C_VDIM

N_BRANCH = 3
BRANCH_W = 1024
IN_SIZES = (2 * A_WIDTH, B_WIDTH, B_WIDTH, B_WIDTH, C_Q_RANK, C_KV_RANK, C_ROPE, N_BRANCH * D_MODEL)
D_IN = 2 * A_WIDTH + 3 * B_WIDTH + C_Q_RANK + C_KV_RANK + C_ROPE + N_BRANCH * D_MODEL

kernel_name = "hybrid_gated_gmlp_diffattn_mla_macaron_dit"


def rms_norm(x, g):
    xf = x.astype(jnp.float32)
    y = xf * lax.rsqrt(jnp.mean(xf * xf, axis=-1, keepdims=True) + EPS)
    return (y * g.astype(jnp.float32)).astype(x.dtype)


def layer_norm(x, g, b):
    xf = x.astype(jnp.float32)
    mu = jnp.mean(xf, axis=-1, keepdims=True)
    xc = xf - mu
    y = xc * lax.rsqrt(jnp.mean(xc * xc, axis=-1, keepdims=True) + EPS)
    return (y * g.astype(jnp.float32) + b.astype(jnp.float32)).astype(x.dtype)


def modulate(h, shift, scale):
    return h * (1.0 + scale) + shift


def swiglu(h, w13, w2):
    a, b = jnp.split(h @ w13, 2, axis=-1)
    return (jax.nn.silu(a) * b) @ w2


def apply_rope(x, cos, sin):
    x1, x2 = jnp.split(x, 2, axis=-1)
    return jnp.concatenate([x1 * cos - x2 * sin, x2 * cos + x1 * sin], axis=-1)


def axial_rope_tables(rows, rot_dim, dtype):
    row = jnp.repeat(jnp.arange(rows), GRID_W)
    col = jnp.tile(jnp.arange(GRID_W), rows)
    n_freq = rot_dim // 4
    freqs = ROPE_THETA ** (-jnp.arange(n_freq, dtype=jnp.float32) / n_freq)
    ang = jnp.concatenate([row[:, None] * freqs, col[:, None] * freqs], axis=-1)
    return jnp.cos(ang).astype(dtype), jnp.sin(ang).astype(dtype)


def sweep_query_blocks(fn, q):
    bsz, n = q.shape[0], q.shape[1]
    nb = n // Q_BLOCK
    qb = jnp.moveaxis(q.reshape((bsz, nb, Q_BLOCK) + q.shape[2:]), 1, 0)
    ob = lax.map(fn, qb)
    return jnp.moveaxis(ob, 0, 1).reshape((bsz, n) + ob.shape[3:])


def chunk_spatial_gating(a, ln_g, ln_b, w_s, b_s):
    u, v = jnp.split(jax.nn.gelu(a), 2, axis=-1)
    v = layer_norm(v, ln_g, ln_b)
    bsz, n = v.shape[0], v.shape[1]
    v = v.reshape(bsz, n // CHUNK, CHUNK, A_GROUPS, A_GROUP_W)
    z = jnp.einsum('gij,bnjgc->bnigc', w_s, v) + b_s.T[:, :, None]
    return u * z.reshape(bsz, n, A_WIDTH)


def diff_softmax_attend(q, k, v, lam):
    s = jnp.einsum('bqhmd,bkhmd->bhmqk', q, k).astype(jnp.float32) * (B_HEAD_DIM ** -0.5)
    p = jax.nn.softmax(s, axis=-1)
    w = p[:, :, 0] - lam * p[:, :, 1]
    return jnp.einsum('bhqk,bkhe->bqhe', w.astype(v.dtype), v)


def diff_post(o, subln_g, lam_init):
    bsz, n = o.shape[0], o.shape[1]
    return (rms_norm(o, subln_g) * (1.0 - lam_init)).reshape(bsz, n, B_WIDTH)


def mla_attend(q, kn, kr, v):
    qn, qr = q[..., :C_NOPE], q[..., C_NOPE:]
    s = (jnp.einsum('bqhd,bkhd->bhqk', qn, kn) + jnp.einsum('bqhr,bkr->bhqk', qr, kr)).astype(jnp.float32)
    p = jax.nn.softmax(s * ((C_NOPE + C_ROPE) ** -0.5), axis=-1)
    return jnp.einsum('bhqk,bkhe->bqhe', p.astype(v.dtype), v)


def gated_merge(oa, ob, oc, g, b_gate, w_branch, w_out):
    ga, gb, gc = jnp.split(jax.nn.sigmoid(g + b_gate), N_BRANCH, axis=-1)
    y = ga * (oa @ w_branch[0]) + gb * (ob @ w_branch[1]) + gc * (oc @ w_branch[2])
    return y @ w_out


def token_mix(hx, hc, w_in, b_gate, ln_v_g, ln_v_b, spatial_w, spatial_b, lam, lam_init, subln_g,
              q_norm_g, w_uq, kv_norm_g, w_ukv, w_branch, w_out, cos_b, sin_b, cos_c, sin_c, ctx_out):
    offsets = [int(o) for o in np.cumsum(IN_SIZES)[:-1]]
    a_x, qb_x, kb_x, vb_x, cq_x, ckv_x, kr_x, g_x = jnp.split(hx @ w_in, offsets, axis=-1)
    a_c, qb_c, kb_c, vb_c, cq_c, ckv_c, kr_c, g_c = jnp.split(hc @ w_in, offsets, axis=-1)

    def diff_heads(q, k, v):
        bsz, L = q.shape[0], q.shape[1]
        return (q.reshape(bsz, L, B_HEADS, 2, B_HEAD_DIM),
                k.reshape(bsz, L, B_HEADS, 2, B_HEAD_DIM),
                v.reshape(bsz, L, B_HEADS, 2 * B_HEAD_DIM))

    def mla_heads(cq, ckv):
        bsz, L = cq.shape[0], cq.shape[1]
        q = (rms_norm(cq, q_norm_g) @ w_uq).reshape(bsz, L, C_HEADS, C_NOPE + C_ROPE)
        kv = (rms_norm(ckv, kv_norm_g) @ w_ukv).reshape(bsz, L, C_HEADS, C_NOPE + C_VDIM)
        return q, kv[..., :C_NOPE], kv[..., C_NOPE:]

    bsz, n = hx.shape[0], hx.shape[1]

    qbx, kbx, vbx = diff_heads(qb_x, kb_x, vb_x)
    rb_cos, rb_sin = cos_b[:, None, None, :], sin_b[:, None, None, :]
    qbx = apply_rope(qbx, rb_cos, rb_sin)
    kbx = apply_rope(kbx, rb_cos, rb_sin)
    qbc, kbc, vbc = diff_heads(qb_c, kb_c, vb_c)
    kb_all = jnp.concatenate([kbc, kbx], axis=1)
    vb_all = jnp.concatenate([vbc, vbx], axis=1)
    ob_x = diff_post(sweep_query_blocks(lambda qb: diff_softmax_attend(qb, kb_all, vb_all, lam), qbx),
                     subln_g, lam_init)

    qcx, kncx, vcx = mla_heads(cq_x, ckv_x)
    qcx = jnp.concatenate([qcx[..., :C_NOPE],
                           apply_rope(qcx[..., C_NOPE:], cos_c[:, None, :], sin_c[:, None, :])], axis=-1)
    krx = apply_rope(kr_x, cos_c, sin_c)
    qcc, kncc, vcc = mla_heads(cq_c, ckv_c)
    kn_all = jnp.concatenate([kncc, kncx], axis=1)
    kr_all = jnp.concatenate([kr_c, krx], axis=1)
    vc_all = jnp.concatenate([vcc, vcx], axis=1)
    oc_x = sweep_query_blocks(lambda qb: mla_attend(qb, kn_all, kr_all, vc_all), qcx).reshape(bsz, n, C_WIDTH)

    oa_x = chunk_spatial_gating(a_x, ln_v_g, ln_v_b, spatial_w, spatial_b)

    out_x = gated_merge(oa_x, ob_x, oc_x, g_x, b_gate, w_branch, w_out)
    if not ctx_out:
        return out_x, None

    ob_c = diff_post(diff_softmax_attend(qbc, kbc, vbc, lam), subln_g, lam_init)
    oc_c = mla_attend(qcc, kncc, kr_c, vcc).reshape(bsz, hc.shape[1], C_WIDTH)
    oa_c = chunk_spatial_gating(a_c, ln_v_g, ln_v_b, spatial_w, spatial_b)
    out_c = gated_merge(oa_c, ob_c, oc_c, g_c, b_gate, w_branch, w_out)
    return out_x, out_c


def setup_inputs(seed: int = 0) -> dict:
    key = jax.random.key(seed)
    ks = jax.random.split(key, 40)
    f32 = jnp.float32

    def nrm(k, shape, fan_in, mult=1.0):
        return jax.random.normal(k, shape, f32) * (mult * fan_in ** -0.5)

    def gain(k, shape):
        return 1.0 + 0.05 * jax.random.normal(k, shape, f32)

    def small(k, shape, s=0.02):
        return s * jax.random.normal(k, shape, f32)

    L = DEPTH
    return {
        "x": jax.random.normal(ks[0], (BATCH, SEQ, D_MODEL), f32),
        "c": jax.random.normal(ks[1], (BATCH, D_MODEL), f32),
        "ctx": jax.random.normal(ks[2], (BATCH, CTX_LEN, D_MODEL), f32),
        "c_ctx": jax.random.normal(ks[3], (D_MODEL,), f32),
        "ada_w": nrm(ks[4], (L, D_MODEL, N_MOD * D_MODEL), D_MODEL, 0.5),
        "ada_b": small(ks[5], (L, N_MOD * D_MODEL)),
        "norm_ffn1": gain(ks[6], (L, D_MODEL)),
        "ffn1_w13": nrm(ks[7], (L, D_MODEL, 2 * D_FF), D_MODEL),
        "ffn1_w2": nrm(ks[8], (L, D_FF, D_MODEL), D_FF),
        "norm_mix": gain(ks[9], (L, D_MODEL)),
        "w_in": nrm(ks[10], (L, D_MODEL, D_IN), D_MODEL),
        "b_gate": small(ks[11], (L, N_BRANCH * D_MODEL), 0.1),
        "ln_v_g": gain(ks[12], (L, A_WIDTH)),
        "ln_v_b": small(ks[13], (L, A_WIDTH)),
        "spatial_w": nrm(ks[14], (L, A_GROUPS, CHUNK, CHUNK), CHUNK, 0.5),
        "spatial_b": gain(ks[15], (L, A_GROUPS, CHUNK)),
        "lambda_q1": small(ks[16], (L, B_HEAD_DIM), 0.1),
        "lambda_k1": small(ks[17], (L, B_HEAD_DIM), 0.1),
        "lambda_q2": small(ks[18], (L, B_HEAD_DIM), 0.1),
        "lambda_k2": small(ks[19], (L, B_HEAD_DIM), 0.1),
        "subln_g": gain(ks[20], (L, 2 * B_HEAD_DIM)),
        "q_norm_g": gain(ks[21], (L, C_Q_RANK)),
        "w_uq": nrm(ks[22], (L, C_Q_RANK, C_HEADS * (C_NOPE + C_ROPE)), C_Q_RANK),
        "kv_norm_g": gain(ks[23], (L, C_KV_RANK)),
        "w_ukv": nrm(ks[24], (L, C_KV_RANK, C_HEADS * (C_NOPE + C_VDIM)), C_KV_RANK),
        "w_branch": nrm(ks[25], (L, N_BRANCH, BRANCH_W, D_MODEL), BRANCH_W),
        "w_out": nrm(ks[26], (L, D_MODEL, D_MODEL), D_MODEL),
        "norm_ffn2": gain(ks[27], (L, D_MODEL)),
        "ffn2_w13": nrm(ks[28], (L, D_MODEL, 2 * D_FF), D_MODEL),
        "ffn2_w2": nrm(ks[29], (L, D_FF, D_MODEL), D_FF),
        "norm_final": gain(ks[30], (D_MODEL,)),
    }


def reference(x, c, ctx, c_ctx, ada_w, ada_b, norm_ffn1, ffn1_w13, ffn1_w2, norm_mix, w_in, b_gate,
              ln_v_g, ln_v_b, spatial_w, spatial_b, lambda_q1, lambda_k1, lambda_q2, lambda_k2, subln_g,
              q_norm_g, w_uq, kv_norm_g, w_ukv, w_branch, w_out, norm_ffn2, ffn2_w13, ffn2_w2, norm_final):
    n = x.shape[1]
    ROWS = n // GRID_W
    cos_b, sin_b = axial_rope_tables(ROWS, B_HEAD_DIM, x.dtype)
    cos_c, sin_c = axial_rope_tables(ROWS, C_ROPE, x.dtype)
    s_lat = jax.nn.silu(c)
    s_ctx = jax.nn.silu(c_ctx)

    for l in range(DEPTH):
        ctx_out = l < DEPTH - 1
        m_x = jnp.split((s_lat @ ada_w[l] + ada_b[l])[:, None, :], N_MOD, axis=-1)
        m_c = jnp.split(s_ctx @ ada_w[l] + ada_b[l], N_MOD, axis=-1)

        x = x + 0.5 * m_x[2] * swiglu(modulate(rms_norm(x, norm_ffn1[l]), m_x[0], m_x[1]), ffn1_w13[l], ffn1_w2[l])
        ctx = ctx + 0.5 * m_c[2] * swiglu(modulate(rms_norm(ctx, norm_ffn1[l]), m_c[0], m_c[1]), ffn1_w13[l], ffn1_w2[l])

        hx = modulate(rms_norm(x, norm_mix[l]), m_x[3], m_x[4])
        hc = modulate(rms_norm(ctx, norm_mix[l]), m_c[3], m_c[4])
        lam_init = 0.8 - 0.6 * math.exp(-0.3 * l)
        lam = (jnp.exp(jnp.sum(lambda_q1[l].astype(jnp.float32) * lambda_k1[l].astype(jnp.float32)))
               - jnp.exp(jnp.sum(lambda_q2[l].astype(jnp.float32) * lambda_k2[l].astype(jnp.float32)))
               + lam_init)
        out_x, out_c = token_mix(hx, hc, w_in[l], b_gate[l], ln_v_g[l], ln_v_b[l], spatial_w[l], spatial_b[l],
                                 lam, lam_init, subln_g[l], q_norm_g[l], w_uq[l], kv_norm_g[l], w_ukv[l],
                                 w_branch[l], w_out[l], cos_b, sin_b, cos_c, sin_c, ctx_out)
        x = x + m_x[5] * out_x

        x = x + 0.5 * m_x[8] * swiglu(modulate(rms_norm(x, norm_ffn2[l]), m_x[6], m_x[7]), ffn2_w13[l], ffn2_w2[l])
        if ctx_out:
            ctx = ctx + m_c[5] * out_c
            ctx = ctx + 0.5 * m_c[8] * swiglu(modulate(rms_norm(ctx, norm_ffn2[l]), m_c[6], m_c[7]), ffn2_w13[l], ffn2_w2[l])

    return rms_norm(x, norm_final)
```

```python
import functools
import itertools
import math

import jax
import jax.numpy as jnp
from jax import lax
from jax.experimental import pallas as pl
from jax.experimental.pallas import tpu as pltpu

F32 = jnp.float32
BF16 = jnp.bfloat16

GRID_W = 64
ROPE_THETA = 10000.0
EPS = 1e-6
N_MOD = 9
CHUNK = 128
A_GROUPS = 8
B_HEADS = 8
B_HEAD_DIM = 64
C_HEADS = 16
C_NOPE = 64
C_ROPE = 32
C_VDIM = 64
C_Q_RANK = 512
C_KV_RANK = 256
LANES = 128
LOG2E = 1.4426950408889634
VMEM_LIMIT = 56 * 1024 * 1024


def _cparams(*sem):
    return pltpu.CompilerParams(dimension_semantics=sem, vmem_limit_bytes=VMEM_LIMIT)


def _resident(shape):
    zeros = (0,) * len(shape)
    return pl.BlockSpec(shape, lambda *_: zeros, pipeline_mode=pl.Buffered(1))


def _pick_tile(cands, *extents):
    for t in cands:
        if all(e % t == 0 for e in extents):
            return t
    raise ValueError(f"no tile in {cands} divides {extents}")


def _rms(x, g):
    return x * lax.rsqrt(jnp.mean(x * x, axis=-1, keepdims=True) + EPS) * g


def _dot(a, b):
    return jnp.dot(a, b, preferred_element_type=F32)


def _dot_nt(a, b):
    return lax.dot_general(a, b, (((1,), (1,)), ((), ())), preferred_element_type=F32)


def _ada_kernel(s_ref, w_ref, b_ref, o_ref):
    s = s_ref[...]
    s = s * jax.nn.sigmoid(s)
    o_ref[...] = jnp.dot(s, w_ref[...], preferred_element_type=F32,
                         precision=lax.Precision.HIGHEST) + b_ref[...]


def _ada_table(cond, ada_w, ada_b):
    depth, d, nd = ada_w.shape
    r = cond.shape[0]
    tn = _pick_tile((1024, 512, 256, 128), nd)
    return pl.pallas_call(
        _ada_kernel,
        grid=(depth, nd // tn),
        in_specs=[
            pl.BlockSpec((r, d), lambda l, j: (0, 0)),
            pl.BlockSpec((None, d, tn), lambda l, j: (l, 0, j)),
            pl.BlockSpec((None, 1, tn), lambda l, j: (l, 0, j)),
        ],
        out_specs=pl.BlockSpec((None, r, tn), lambda l, j: (l, 0, j)),
        out_shape=jax.ShapeDtypeStruct((depth, r, nd), F32),
        compiler_params=_cparams("parallel", "parallel"),
        name="ada_table",
    )(cond, ada_w, ada_b.reshape(depth, 1, nd))


def _ffn_kernel(x_ref, mod_ref, ng_ref, w1_ref, w3_ref, w2_ref, ng2_ref, o_ref, *h_refs, k0, post):
    x = x_ref[...]
    h = _rms(x, ng_ref[...]) * (1.0 + mod_ref[k0 + 1:k0 + 2, :]) + mod_ref[k0:k0 + 1, :]
    hb = h.astype(BF16)
    a = _dot(hb, w1_ref[...])
    b = _dot(hb, w3_ref[...])
    act = (a * jax.nn.sigmoid(a) * b).astype(BF16)
    y = x + (0.5 * mod_ref[k0 + 2:k0 + 3, :]) * _dot(act, w2_ref[...])
    if post == "final":
        o_ref[...] = _rms(y, ng2_ref[...])
    else:
        o_ref[...] = y
    if post == "mix":
        hx = _rms(y, ng2_ref[...]) * (1.0 + mod_ref[4:5, :]) + mod_ref[3:4, :]
        h_refs[0][...] = hx.astype(BF16)


def _ffn(x, mod, ng, w1, w3, w2, ng2, *, k0, post, rows, seq, n_batch):
    d = x.shape[1]
    dff = w1.shape[1]
    tm = _pick_tile((512, 256, 128), rows, seq)
    mod_idx = lambda i: (jnp.minimum(i * tm // seq, n_batch), 0, 0)
    out_shape = [jax.ShapeDtypeStruct((rows, d), F32)]
    out_specs = [pl.BlockSpec((tm, d), lambda i: (i, 0))]
    if post == "mix":
        out_shape.append(jax.ShapeDtypeStruct((rows, d), BF16))
        out_specs.append(pl.BlockSpec((tm, d), lambda i: (i, 0)))
    res = pl.pallas_call(
        functools.partial(_ffn_kernel, k0=k0, post=post),
        grid=(rows // tm,),
        in_specs=[
            pl.BlockSpec((tm, d), lambda i: (i, 0)),
            pl.BlockSpec((None, N_MOD, d), mod_idx),
            _resident((1, d)),
            _resident((d, dff)),
            _resident((d, dff)),
            _resident((dff, d)),
            _resident((1, d)),
        ],
        out_specs=out_specs,
        out_shape=out_shape,
        compiler_params=_cparams("parallel"),
        name="ffn_" + str(post),
    )(x, mod, ng, w1, w3, w2, ng2)
    return res if post == "mix" else res[0]


def _gmlp_kernel(h_ref, wa_ref, lng_ref, lnb_ref, ws_ref, bs_ref, o_ref):
    tm, aw = o_ref.shape
    a = _dot(h_ref[...], wa_ref[...])
    a = 0.5 * a * (1.0 + jnp.tanh(math.sqrt(2.0 / math.pi) * (a + 0.044715 * (a * a * a))))
    u = a[:, :aw]
    v = a[:, aw:]
    mu = jnp.mean(v, axis=-1, keepdims=True)
    vc = v - mu
    vn = vc * lax.rsqrt(jnp.mean(vc * vc, axis=-1, keepdims=True) + EPS) * lng_ref[...] + lnb_ref[...]
    vb = vn.astype(BF16)
    gw = aw // A_GROUPS
    for r in range(tm // CHUNK):
        rows = slice(r * CHUNK, (r + 1) * CHUNK)
        for g in range(A_GROUPS):
            cols = slice(g * gw, (g + 1) * gw)
            z = _dot(ws_ref[g], vb[rows, cols]) + bs_ref[:, cols]
            o_ref[rows, cols] = (u[rows, cols] * z).astype(BF16)


def _gmlp(hx, wa, lng, lnb, ws, bs, *, rows):
    d = hx.shape[1]
    aw = wa.shape[1] // 2
    tm = _pick_tile((512, 256, 128), rows)
    return pl.pallas_call(
        _gmlp_kernel,
        grid=(rows // tm,),
        in_specs=[
            pl.BlockSpec((tm, d), lambda i: (i, 0)),
            _resident((d, 2 * aw)),
            _resident((1, aw)),
            _resident((1, aw)),
            _resident((A_GROUPS, CHUNK, CHUNK)),
            _resident((CHUNK, aw)),
        ],
        out_specs=pl.BlockSpec((tm, aw), lambda i: (i, 0)),
        out_shape=jax.ShapeDtypeStruct((rows, aw), BF16),
        compiler_params=_cparams("parallel"),
        name="mix_gmlp",
    )(hx, wa, lng, lnb, ws, bs)


def _rope(x, tab_ref, half):
    return (x * tab_ref[0] + pltpu.roll(x, LANES - half, 1) * tab_ref[1]
            + pltpu.roll(x, half, 1) * tab_ref[2])


def _rope_tables(seq, tm, rot_dim, lane_lo, period):
    rows = seq // GRID_W
    row = jnp.repeat(jnp.arange(rows), GRID_W)
    col = jnp.tile(jnp.arange(GRID_W), rows)
    n_freq = rot_dim // 4
    freqs = ROPE_THETA ** (-jnp.arange(n_freq, dtype=F32) / n_freq)
    ang = jnp.concatenate([row[:, None] * freqs, col[:, None] * freqs], axis=-1)
    cos, sin = jnp.cos(ang), jnp.sin(ang)
    half = rot_dim // 2
    c = jnp.ones((seq, LANES), F32)
    s1 = jnp.zeros((seq, LANES), F32)
    s2 = jnp.zeros((seq, LANES), F32)
    for lo in range(lane_lo, LANES, period):
        if lo + rot_dim > LANES:
            break
        c = c.at[:, lo:lo + half].set(cos).at[:, lo + half:lo + rot_dim].set(cos)
        s1 = s1.at[:, lo:lo + half].set(-sin)
        s2 = s2.at[:, lo + half:lo + rot_dim].set(sin)
    tab = jnp.stack([c, s1, s2])
    ident = jnp.stack([jnp.ones((tm, LANES), F32), jnp.zeros((tm, LANES), F32), jnp.zeros((tm, LANES), F32)])
    return jnp.concatenate([tab, ident], axis=1)


def _projb_kernel(h_ref, w_ref, tab_ref, q_ref, k_ref, v_ref, *, qscale):
    r = _dot(h_ref[...], w_ref[...])
    bw = B_HEADS * LANES
    half = B_HEAD_DIM // 2
    for hh in range(B_HEADS):
        cols = slice(hh * LANES, (hh + 1) * LANES)
        q_ref[hh] = (_rope(r[:, cols], tab_ref, half) * qscale).astype(BF16)
        k_ref[hh] = _rope(r[:, bw + hh * LANES:bw + (hh + 1) * LANES], tab_ref, half).astype(BF16)
        v_ref[hh] = r[:, 2 * bw + hh * LANES:2 * bw + (hh + 1) * LANES].astype(BF16)


def _tab_index(tm, n_lat_tiles, tiles_per_seq):
    return lambda i: (0, jnp.where(i < n_lat_tiles, i % tiles_per_seq, tiles_per_seq), 0)


def _projb(hx, w, tab, *, tm, n_lat, seq):
    t, d = hx.shape
    hm = jax.ShapeDtypeStruct((B_HEADS, t, LANES), BF16)
    hm_spec = pl.BlockSpec((B_HEADS, tm, LANES), lambda i: (0, i, 0))
    return pl.pallas_call(
        functools.partial(_projb_kernel, qscale=B_HEAD_DIM ** -0.5 * LOG2E),
        grid=(t // tm,),
        in_specs=[
            pl.BlockSpec((tm, d), lambda i: (i, 0)),
            _resident(w.shape),
            pl.BlockSpec((3, tm, LANES), _tab_index(tm, n_lat // tm, seq // tm)),
        ],
        out_specs=[hm_spec, hm_spec, hm_spec],
        out_shape=[hm, hm, hm],
        compiler_params=_cparams("parallel"),
        name="mix_projb",
    )(hx, w, tab)


def _projc_kernel(h_ref, w1_ref, qg_ref, kvg_ref, wuq_ref, wuk_ref, wuv_ref, tab_ref,
                  q_ref, k_ref, v_ref, *, qscale):
    r = _dot(h_ref[...], w1_ref[...])
    cq = _rms(r[:, :C_Q_RANK], qg_ref[...]).astype(BF16)
    ckv = _rms(r[:, C_Q_RANK:C_Q_RANK + C_KV_RANK], kvg_ref[...]).astype(BF16)
    half = C_ROPE // 2
    kr = _rope(r[:, C_Q_RANK + C_KV_RANK:], tab_ref, half)
    q = _dot(cq, wuq_ref[...])
    kn = _dot(ckv, wuk_ref[...])
    v = _dot(ckv, wuv_ref[...])
    for hh in range(C_HEADS):
        cols = slice(hh * LANES, (hh + 1) * LANES)
        q_ref[hh] = (_rope(q[:, cols], tab_ref, half) * qscale).astype(BF16)
        k_ref[hh] = (kn[:, cols] + kr).astype(BF16)
    for p in range(C_HEADS // 2):
        v_ref[p] = v[:, p * LANES:(p + 1) * LANES].astype(BF16)


def _projc(hx, w1, qg, kvg, wuq, wuk, wuv, tab, *, tm, n_lat, seq):
    t, d = hx.shape
    qk = jax.ShapeDtypeStruct((C_HEADS, t, LANES), BF16)
    qk_spec = pl.BlockSpec((C_HEADS, tm, LANES), lambda i: (0, i, 0))
    return pl.pallas_call(
        functools.partial(_projc_kernel, qscale=(C_NOPE + C_ROPE) ** -0.5 * LOG2E),
        grid=(t // tm,),
        in_specs=[
            pl.BlockSpec((tm, d), lambda i: (i, 0)),
            _resident(w1.shape), _resident(qg.shape), _resident(kvg.shape),
            _resident(wuq.shape), _resident(wuk.shape), _resident(wuv.shape),
            pl.BlockSpec((3, tm, LANES), _tab_index(tm, n_lat // tm, seq // tm)),
        ],
        out_specs=[qk_spec, qk_spec, pl.BlockSpec((C_HEADS // 2, tm, LANES), lambda i: (0, i, 0))],
        out_shape=[qk, qk, jax.ShapeDtypeStruct((C_HEADS // 2, t, LANES), BF16)],
        compiler_params=_cparams("parallel"),
        name="mix_projc",
    )(hx, w1, qg, kvg, wuq, wuk, wuv, tab)


def _softmax_parts(q, k_refs, sel):
    s = [_dot_nt(q, sel(k)) for k in k_refs]
    m = functools.reduce(jnp.maximum, [jnp.max(x, axis=-1, keepdims=True) for x in s])
    e = [jnp.exp2(x - m) for x in s]
    l = functools.reduce(jnp.add, [jnp.sum(x, axis=-1, keepdims=True) for x in e])
    return e, l


def _diff_attn_kernel(*refs, n_seg, rs, lam_init):
    lamv_ref, g_ref, q_ref = refs[:3]
    k_refs = refs[3:3 + n_seg]
    v_refs = refs[3 + n_seg:3 + 2 * n_seg]
    o_ref = refs[3 + 2 * n_seg]
    lv = lamv_ref[...]
    lam = (jnp.exp(jnp.sum(lv[0:1] * lv[1:2], axis=-1, keepdims=True))
           - jnp.exp(jnp.sum(lv[2:3] * lv[3:4], axis=-1, keepdims=True)) + lam_init)
    for r in range(q_ref.shape[0] // rs):
        rows = slice(r * rs, (r + 1) * rs)
        q = q_ref[rows, :]
        lane = lax.broadcasted_iota(jnp.int32, q.shape, 1)
        zero = jnp.zeros_like(q)
        e1, l1 = _softmax_parts(jnp.where(lane < B_HEAD_DIM, q, zero), k_refs, lambda k: k[...])
        e2, l2 = _softmax_parts(jnp.where(lane >= B_HEAD_DIM, q, zero), k_refs, lambda k: k[...])
        c1 = 1.0 / l1
        c2 = lam / l2
        o = functools.reduce(jnp.add, [_dot((a * c1 - b * c2).astype(BF16), v[...])
                                       for a, b, v in zip(e1, e2, v_refs)])
        o_ref[rows, :] = (_rms(o, g_ref[...]) * (1.0 - lam_init)).astype(BF16)


def _mla_attn_kernel(*refs, n_seg, rs):
    q_ref = refs[0]
    k_refs = refs[1:1 + n_seg]
    v_refs = refs[1 + n_seg:1 + 2 * n_seg]
    o_ref = refs[1 + 2 * n_seg]
    for r in range(q_ref.shape[1] // rs):
        rows = slice(r * rs, (r + 1) * rs)
        outs = []
        for hh in range(2):
            e, l = _softmax_parts(q_ref[hh, rows, :], k_refs, lambda k: k[hh])
            pv = functools.reduce(jnp.add, [_dot(a.astype(BF16), v[...]) for a, v in zip(e, v_refs)])
            outs.append(pv / l)
        lane = lax.broadcasted_iota(jnp.int32, outs[0].shape, 1)
        o_ref[rows, :] = jnp.where(lane < C_VDIM, outs[0], outs[1]).astype(BF16)


def _attn_specs(hb, tq, nq, q_blk0, segs):
    lead = (None,) if hb == 1 else (hb,)
    q_spec = pl.BlockSpec(lead + (tq, LANES), lambda b, h, i: (h, q_blk0 + b * nq + i, 0))
    k_specs = [pl.BlockSpec(lead + (n, LANES), functools.partial(lambda b, h, i, o: (h, o + b, 0), o=o))
               for n, o in segs]
    v_specs = [pl.BlockSpec((None, n, LANES), functools.partial(lambda b, h, i, o: (h, o + b, 0), o=o))
               for n, o in segs]
    return q_spec, k_specs, v_specs


def _diff_attn(lamv, g, q, k, v, *, n_batch, lq, q_row0, segs, lam_init, alias=None):
    t = q.shape[1]
    tq = _pick_tile((512, 256, 128), lq)
    rs = min(tq, 256)
    nq = lq // tq
    n_seg = len(segs)
    q_spec, k_specs, v_specs = _attn_specs(1, tq, nq, q_row0 // tq, segs)
    in_specs = [_resident(lamv.shape), _resident(g.shape), q_spec] + k_specs + v_specs
    args = [lamv, g, q] + [k] * n_seg + [v] * n_seg
    io_alias = {}
    if alias is not None:
        in_specs.append(pl.BlockSpec(memory_space=pl.ANY))
        args.append(alias)
        io_alias = {len(args) - 1: 0}
    kern = functools.partial(_diff_attn_kernel, n_seg=n_seg, rs=rs, lam_init=lam_init)
    if alias is not None:
        kern = functools.partial(_drop_alias_arg, kern, 3 + 2 * n_seg)
    return pl.pallas_call(
        kern,
        grid=(n_batch, B_HEADS, nq),
        in_specs=in_specs,
        out_specs=pl.BlockSpec((tq, LANES), lambda b, h, i: (q_row0 // tq + b * nq + i, h)),
        out_shape=jax.ShapeDtypeStruct((t, B_HEADS * LANES), BF16),
        input_output_aliases=io_alias,
        compiler_params=_cparams("parallel", "parallel", "arbitrary"),
        name="attn_diff",
    )(*args)


def _drop_alias_arg(kern, pos, *refs):
    return kern(*(refs[:pos] + refs[pos + 1:]))


def _mla_attn(q, k, v, *, n_batch, lq, q_row0, segs, alias=None):
    t = q.shape[1]
    tq = _pick_tile((512, 256, 128), lq)
    rs = min(tq, 256)
    nq = lq // tq
    n_seg = len(segs)
    q_spec, k_specs, v_specs = _attn_specs(2, tq, nq, q_row0 // tq, segs)
    in_specs = [q_spec] + k_specs + v_specs
    args = [q] + [k] * n_seg + [v] * n_seg
    io_alias = {}
    kern = functools.partial(_mla_attn_kernel, n_seg=n_seg, rs=rs)
    if alias is not None:
        in_specs.append(pl.BlockSpec(memory_space=pl.ANY))
        args.append(alias)
        io_alias = {len(args) - 1: 0}
        kern = functools.partial(_drop_alias_arg, kern, 1 + 2 * n_seg)
    return pl.pallas_call(
        kern,
        grid=(n_batch, C_HEADS // 2, nq),
        in_specs=in_specs,
        out_specs=pl.BlockSpec((tq, LANES), lambda b, h, i: (q_row0 // tq + b * nq + i, h)),
        out_shape=jax.ShapeDtypeStruct((t, C_HEADS * C_VDIM), BF16),
        input_output_aliases=io_alias,
        compiler_params=_cparams("parallel", "parallel", "arbitrary"),
        name="attn_mla",
    )(*args)


def _merge_kernel(x_ref, mod_ref, h_ref, oa_ref, ob_ref, oc_ref, wg_ref, bg_ref, wb_ref, wo_ref, o_ref):
    d = x_ref.shape[1]
    g = jax.nn.sigmoid(_dot(h_ref[...], wg_ref[...]) + bg_ref[...])
    y = (g[:, :d] * _dot(oa_ref[...], wb_ref[0]) + g[:, d:2 * d] * _dot(ob_ref[...], wb_ref[1])
         + g[:, 2 * d:] * _dot(oc_ref[...], wb_ref[2]))
    o_ref[...] = x_ref[...] + mod_ref[5:6, :] * _dot(y.astype(BF16), wo_ref[...])


def _merge(x, mod, hx, oa, ob, oc, wg, bg, wb, wo, *, rows, seq, n_batch):
    d = x.shape[1]
    tm = _pick_tile((512, 256, 128), rows, seq)
    row_spec = pl.BlockSpec((tm, d), lambda i: (i, 0))
    return pl.pallas_call(
        _merge_kernel,
        grid=(rows // tm,),
        in_specs=[
            row_spec,
            pl.BlockSpec((None, N_MOD, d), lambda i: (jnp.minimum(i * tm // seq, n_batch), 0, 0)),
            row_spec, row_spec, row_spec, row_spec,
            _resident(wg.shape), _resident(bg.shape), _resident(wb.shape), _resident(wo.shape),
        ],
        out_specs=row_spec,
        out_shape=jax.ShapeDtypeStruct((rows, d), F32),
        compiler_params=_cparams("parallel"),
        name="mix_merge",
    )(x, mod, hx, oa, ob, oc, wg, bg, wb, wo)


def kernel(x, c, ctx, c_ctx, ada_w, ada_b, norm_ffn1, ffn1_w13, ffn1_w2, norm_mix, w_in, b_gate, ln_v_g, ln_v_b, spatial_w, spatial_b, lambda_q1, lambda_k1, lambda_q2, lambda_k2, subln_g, q_norm_g, w_uq, kv_norm_g, w_ukv, w_branch, w_out, norm_ffn2, ffn2_w13, ffn2_w2, norm_final):
    n_batch, seq, d = x.shape
    n_ctx = ctx.shape[1]
    depth = ada_w.shape[0]
    dff = ffn1_w2.shape[1]
    n_lat = n_batch * seq
    n_all = n_lat + n_batch * n_ctx
    aw = ln_v_g.shape[1]
    bw = B_HEADS * 2 * B_HEAD_DIM

    n_cond = -(-(n_batch + 1) // 8) * 8
    cond = jnp.zeros((n_cond, d), F32).at[:n_batch].set(c).at[n_batch].set(c_ctx)
    mod_all = _ada_table(cond, ada_w, ada_b)[:, :n_batch + 1].reshape(depth, n_batch + 1, N_MOD, d)

    xs = jnp.concatenate([x.reshape(n_lat, d), ctx.reshape(n_batch * n_ctx, d)], axis=0)

    tp = _pick_tile((512, 256, 128), seq, n_batch * n_ctx)
    tab_b = _rope_tables(seq, tp, B_HEAD_DIM, 0, B_HEAD_DIM)
    tab_c = _rope_tables(seq, tp, C_ROPE, C_NOPE, LANES)

    o_a, o_q, o_k, o_v, o_cq, o_ckv, o_kr = itertools.accumulate(
        [2 * aw, bw, bw, bw, C_Q_RANK, C_KV_RANK, C_ROPE])
    row = lambda v: v.reshape(1, -1)
    lat_seg = (seq, 0)
    ctx_seg = (n_ctx, n_lat // n_ctx)

    for l in range(depth):
        last = l == depth - 1
        rows = n_lat if last else n_all
        mod = mod_all[l]
        lam_init = 0.8 - 0.6 * math.exp(-0.3 * l)

        wi = w_in[l]
        w_a = wi[:, :o_a].astype(BF16)
        w_qkv = wi[:, o_a:o_v].astype(BF16)
        w_kr = jnp.zeros((d, LANES), F32).at[:, C_NOPE:C_NOPE + C_ROPE].set(wi[:, o_ckv:o_kr])
        w_c1 = jnp.concatenate([wi[:, o_v:o_ckv], w_kr], axis=1).astype(BF16)
        w_g = wi[:, o_kr:].astype(BF16)
        wuq = jnp.pad(w_uq[l].reshape(C_Q_RANK, C_HEADS, C_NOPE + C_ROPE),
                      ((0, 0), (0, 0), (0, LANES - C_NOPE - C_ROPE))).reshape(C_Q_RANK, C_HEADS * LANES).astype(BF16)
        wukv = w_ukv[l].reshape(C_KV_RANK, C_HEADS, C_NOPE + C_VDIM)
        wuk = jnp.pad(wukv[:, :, :C_NOPE], ((0, 0), (0, 0), (0, LANES - C_NOPE))).reshape(
            C_KV_RANK, C_HEADS * LANES).astype(BF16)
        wuv = wukv[:, :, C_NOPE:].reshape(C_KV_RANK, C_HEADS * C_VDIM).astype(BF16)
        bs_full = jnp.repeat(spatial_b[l].T, aw // A_GROUPS, axis=1)
        lamv = jnp.stack([lambda_q1[l], lambda_k1[l], lambda_q2[l], lambda_k2[l]])

        xs, hx = _ffn(xs, mod, row(norm_ffn1[l]), ffn1_w13[l][:, :dff].astype(BF16),
                      ffn1_w13[l][:, dff:].astype(BF16), ffn1_w2[l].astype(BF16), row(norm_mix[l]),
                      k0=0, post="mix", rows=n_all, seq=seq, n_batch=n_batch)

        oa = _gmlp(hx, w_a, row(ln_v_g[l]), row(ln_v_b[l]), spatial_w[l].astype(BF16), bs_full, rows=rows)
        qb, kb, vb = _projb(hx, w_qkv, tab_b, tm=tp, n_lat=n_lat, seq=seq)
        qc, kc, vc = _projc(hx, w_c1, row(q_norm_g[l]), row(kv_norm_g[l]), wuq, wuk, wuv, tab_c,
                            tm=tp, n_lat=n_lat, seq=seq)

        g_sub = row(subln_g[l])
        ob = _diff_attn(lamv, g_sub, qb, kb, vb, n_batch=n_batch, lq=seq, q_row0=0,
                        segs=[lat_seg, ctx_seg], lam_init=lam_init)
        oc = _mla_attn(qc, kc, vc, n_batch=n_batch, lq=seq, q_row0=0, segs=[lat_seg, ctx_seg])
        if not last:
            ob = _diff_attn(lamv, g_sub, qb, kb, vb, n_batch=n_batch, lq=n_ctx, q_row0=n_lat,
                            segs=[ctx_seg], lam_init=lam_init, alias=ob)
            oc = _mla_attn(qc, kc, vc, n_batch=n_batch, lq=n_ctx, q_row0=n_lat, segs=[ctx_seg], alias=oc)

        xs = _merge(xs, mod, hx, oa, ob, oc, w_g, row(b_gate[l]), w_branch[l].astype(BF16),
                    w_out[l].astype(BF16), rows=rows, seq=seq, n_batch=n_batch)

        xs = _ffn(xs, mod, row(norm_ffn2[l]), ffn2_w13[l][:, :dff].astype(BF16),
                  ffn2_w13[l][:, dff:].astype(BF16), ffn2_w2[l].astype(BF16), row(norm_final),
                  k0=6, post="final" if last else None, rows=rows, seq=seq, n_batch=n_batch)

    return xs[:n_lat].reshape(n_batch, seq, d)
```

```python
import functools
import itertools
import math

import jax
import jax.numpy as jnp
from jax import lax
from jax.experimental import pallas as pl
from jax.experimental.pallas import tpu as pltpu

F32 = jnp.float32
BF16 = jnp.bfloat16

GRID_W = 64
ROPE_THETA = 10000.0
EPS = 1e-6
N_MOD = 9
CHUNK = 128
A_GROUPS = 8
B_HEADS = 8
B_HEAD_DIM = 64
C_HEADS = 16
C_NOPE = 64
C_ROPE = 32
C_VDIM = 64
C_Q_RANK = 512
C_KV_RANK = 256
LANES = 128
LOG2E = 1.4426950408889634
KEY_CHUNK = 512
VMEM_LIMIT = 56 * 1024 * 1024


def _cparams(*sem):
    return pltpu.CompilerParams(dimension_semantics=sem, vmem_limit_bytes=VMEM_LIMIT)


def _resident(shape):
    zeros = (0,) * len(shape)
    return pl.BlockSpec(shape, lambda *_: zeros, pipeline_mode=pl.Buffered(1))


def _pick_tile(cands, *extents):
    for t in cands:
        if all(e % t == 0 for e in extents):
            return t
    raise ValueError(f"no tile in {cands} divides {extents}")


def _rms(x, g):
    return x * lax.rsqrt(jnp.mean(x * x, axis=-1, keepdims=True) + EPS) * g


def _dot(a, b):
    return jnp.dot(a, b, preferred_element_type=F32)


def _dot_nt(a, b):
    return lax.dot_general(a, b, (((1,), (1,)), ((), ())), preferred_element_type=F32)


def _ada_kernel(s_ref, w_ref, b_ref, o_ref):
    s = s_ref[...]
    s = s * jax.nn.sigmoid(s)
    o_ref[...] = jnp.dot(s, w_ref[...], preferred_element_type=F32,
                         precision=lax.Precision.HIGHEST) + b_ref[...]


def _ada_table(cond, ada_w, ada_b):
    depth, d, nd = ada_w.shape
    r = cond.shape[0]
    tn = _pick_tile((1024, 512, 256, 128), nd)
    return pl.pallas_call(
        _ada_kernel,
        grid=(depth, nd // tn),
        in_specs=[
            pl.BlockSpec((r, d), lambda l, j: (0, 0)),
            pl.BlockSpec((None, d, tn), lambda l, j: (l, 0, j)),
            pl.BlockSpec((None, 1, tn), lambda l, j: (l, 0, j)),
        ],
        out_specs=pl.BlockSpec((None, r, tn), lambda l, j: (l, 0, j)),
        out_shape=jax.ShapeDtypeStruct((depth, r, nd), F32),
        compiler_params=_cparams("parallel", "parallel"),
        name="ada_table",
    )(cond, ada_w, ada_b.reshape(depth, 1, nd))


def _ffn_kernel(x_ref, mod_ref, ng_ref, w1_ref, w3_ref, w2_ref, ng2_ref, o_ref, *h_refs, k0, post):
    x = x_ref[...]
    h = _rms(x, ng_ref[...]) * (1.0 + mod_ref[k0 + 1:k0 + 2, :]) + mod_ref[k0:k0 + 1, :]
    hb = h.astype(BF16)
    a = _dot(hb, w1_ref[...])
    b = _dot(hb, w3_ref[...])
    act = (a * jax.nn.sigmoid(a) * b).astype(BF16)
    y = x + (0.5 * mod_ref[k0 + 2:k0 + 3, :]) * _dot(act, w2_ref[...])
    if post == "final":
        o_ref[...] = _rms(y, ng2_ref[...])
    else:
        o_ref[...] = y
    if post == "mix":
        hx = _rms(y, ng2_ref[...]) * (1.0 + mod_ref[4:5, :]) + mod_ref[3:4, :]
        h_refs[0][...] = hx.astype(BF16)


def _ffn(x, mod, ng, w1, w3, w2, ng2, *, k0, post, rows, seq, n_batch):
    d = x.shape[1]
    dff = w1.shape[1]
    tm = _pick_tile((512, 256, 128), rows, seq)
    mod_idx = lambda i: (jnp.minimum(i * tm // seq, n_batch), 0, 0)
    out_shape = [jax.ShapeDtypeStruct((rows, d), F32)]
    out_specs = [pl.BlockSpec((tm, d), lambda i: (i, 0))]
    if post == "mix":
        out_shape.append(jax.ShapeDtypeStruct((rows, d), BF16))
        out_specs.append(pl.BlockSpec((tm, d), lambda i: (i, 0)))
    res = pl.pallas_call(
        functools.partial(_ffn_kernel, k0=k0, post=post),
        grid=(rows // tm,),
        in_specs=[
            pl.BlockSpec((tm, d), lambda i: (i, 0)),
            pl.BlockSpec((None, N_MOD, d), mod_idx),
            _resident((1, d)),
            _resident((d, dff)),
            _resident((d, dff)),
            _resident((dff, d)),
            _resident((1, d)),
        ],
        out_specs=out_specs,
        out_shape=out_shape,
        compiler_params=_cparams("parallel"),
        name="ffn_" + str(post),
    )(x, mod, ng, w1, w3, w2, ng2)
    return res if post == "mix" else res[0]


def _gmlp_kernel(h_ref, wa_ref, lng_ref, lnb_ref, ws_ref, bs_ref, o_ref):
    tm, aw = o_ref.shape
    a = _dot(h_ref[...], wa_ref[...])
    a = 0.5 * a * (1.0 + jnp.tanh(math.sqrt(2.0 / math.pi) * (a + 0.044715 * (a * a * a))))
    u = a[:, :aw]
    v = a[:, aw:]
    mu = jnp.mean(v, axis=-1, keepdims=True)
    vc = v - mu
    vn = vc * lax.rsqrt(jnp.mean(vc * vc, axis=-1, keepdims=True) + EPS) * lng_ref[...] + lnb_ref[...]
    vb = vn.astype(BF16)
    gw = aw // A_GROUPS
    for r in range(tm // CHUNK):
        rows = slice(r * CHUNK, (r + 1) * CHUNK)
        for g in range(A_GROUPS):
            cols = slice(g * gw, (g + 1) * gw)
            z = _dot(ws_ref[g], vb[rows, cols]) + bs_ref[:, cols]
            o_ref[rows, cols] = (u[rows, cols] * z).astype(BF16)


def _gmlp(hx, wa, lng, lnb, ws, bs, *, rows):
    d = hx.shape[1]
    aw = wa.shape[1] // 2
    tm = _pick_tile((512, 256, 128), rows)
    return pl.pallas_call(
        _gmlp_kernel,
        grid=(rows // tm,),
        in_specs=[
            pl.BlockSpec((tm, d), lambda i: (i, 0)),
            _resident((d, 2 * aw)),
            _resident((1, aw)),
            _resident((1, aw)),
            _resident((A_GROUPS, CHUNK, CHUNK)),
            _resident((CHUNK, aw)),
        ],
        out_specs=pl.BlockSpec((tm, aw), lambda i: (i, 0)),
        out_shape=jax.ShapeDtypeStruct((rows, aw), BF16),
        compiler_params=_cparams("parallel"),
        name="mix_gmlp",
    )(hx, wa, lng, lnb, ws, bs)


def _rope(x, tab_ref, half):
    return (x * tab_ref[0] + pltpu.roll(x, LANES - half, 1) * tab_ref[1]
            + pltpu.roll(x, half, 1) * tab_ref[2])


def _rope_tables(seq, tm, rot_dim, lane_lo, period):
    rows = seq // GRID_W
    row = jnp.repeat(jnp.arange(rows), GRID_W)
    col = jnp.tile(jnp.arange(GRID_W), rows)
    n_freq = rot_dim // 4
    freqs = ROPE_THETA ** (-jnp.arange(n_freq, dtype=F32) / n_freq)
    ang = jnp.concatenate([row[:, None] * freqs, col[:, None] * freqs], axis=-1)
    cos, sin = jnp.cos(ang), jnp.sin(ang)
    half = rot_dim // 2
    c = jnp.ones((seq, LANES), F32)
    s1 = jnp.zeros((seq, LANES), F32)
    s2 = jnp.zeros((seq, LANES), F32)
    for lo in range(lane_lo, LANES, period):
        if lo + rot_dim > LANES:
            break
        c = c.at[:, lo:lo + half].set(cos).at[:, lo + half:lo + rot_dim].set(cos)
        s1 = s1.at[:, lo:lo + half].set(-sin)
        s2 = s2.at[:, lo + half:lo + rot_dim].set(sin)
    tab = jnp.stack([c, s1, s2])
    ident = jnp.stack([jnp.ones((tm, LANES), F32), jnp.zeros((tm, LANES), F32), jnp.zeros((tm, LANES), F32)])
    return jnp.concatenate([tab, ident], axis=1)


def _projb_kernel(h_ref, w_ref, tab_ref, q_ref, k_ref, v_ref, *, qscale):
    r = _dot(h_ref[...], w_ref[...])
    bw = B_HEADS * LANES
    half = B_HEAD_DIM // 2
    for hh in range(B_HEADS):
        cols = slice(hh * LANES, (hh + 1) * LANES)
        q_ref[hh] = (_rope(r[:, cols], tab_ref, half) * qscale).astype(BF16)
        k_ref[hh] = _rope(r[:, bw + hh * LANES:bw + (hh + 1) * LANES], tab_ref, half).astype(BF16)
        v_ref[hh] = r[:, 2 * bw + hh * LANES:2 * bw + (hh + 1) * LANES].astype(BF16)


def _tab_index(tm, n_lat_tiles, tiles_per_seq):
    return lambda i: (0, jnp.where(i < n_lat_tiles, i % tiles_per_seq, tiles_per_seq), 0)


def _projb(hx, w, tab, *, tm, n_lat, seq):
    t, d = hx.shape
    hm = jax.ShapeDtypeStruct((B_HEADS, t, LANES), BF16)
    hm_spec = pl.BlockSpec((B_HEADS, tm, LANES), lambda i: (0, i, 0))
    return pl.pallas_call(
        functools.partial(_projb_kernel, qscale=B_HEAD_DIM ** -0.5 * LOG2E),
        grid=(t // tm,),
        in_specs=[
            pl.BlockSpec((tm, d), lambda i: (i, 0)),
            _resident(w.shape),
            pl.BlockSpec((3, tm, LANES), _tab_index(tm, n_lat // tm, seq // tm)),
        ],
        out_specs=[hm_spec, hm_spec, hm_spec],
        out_shape=[hm, hm, hm],
        compiler_params=_cparams("parallel"),
        name="mix_projb",
    )(hx, w, tab)


def _projc_kernel(h_ref, w1_ref, qg_ref, kvg_ref, wuq_ref, wuk_ref, wuv_ref, tab_ref,
                  q_ref, k_ref, v_ref, *, qscale):
    r = _dot(h_ref[...], w1_ref[...])
    cq = _rms(r[:, :C_Q_RANK], qg_ref[...]).astype(BF16)
    ckv = _rms(r[:, C_Q_RANK:C_Q_RANK + C_KV_RANK], kvg_ref[...]).astype(BF16)
    half = C_ROPE // 2
    kr = _rope(r[:, C_Q_RANK + C_KV_RANK:], tab_ref, half)
    q = _dot(cq, wuq_ref[...])
    kn = _dot(ckv, wuk_ref[...])
    v = _dot(ckv, wuv_ref[...])
    for hh in range(C_HEADS):
        cols = slice(hh * LANES, (hh + 1) * LANES)
        q_ref[hh] = (_rope(q[:, cols], tab_ref, half) * qscale).astype(BF16)
        k_ref[hh] = (kn[:, cols] + kr).astype(BF16)
    for p in range(C_HEADS // 2):
        v_ref[p] = v[:, p * LANES:(p + 1) * LANES].astype(BF16)


def _projc(hx, w1, qg, kvg, wuq, wuk, wuv, tab, *, tm, n_lat, seq):
    t, d = hx.shape
    qk = jax.ShapeDtypeStruct((C_HEADS, t, LANES), BF16)
    qk_spec = pl.BlockSpec((C_HEADS, tm, LANES), lambda i: (0, i, 0))
    return pl.pallas_call(
        functools.partial(_projc_kernel, qscale=(C_NOPE + C_ROPE) ** -0.5 * LOG2E),
        grid=(t // tm,),
        in_specs=[
            pl.BlockSpec((tm, d), lambda i: (i, 0)),
            _resident(w1.shape), _resident(qg.shape), _resident(kvg.shape),
            _resident(wuq.shape), _resident(wuk.shape), _resident(wuv.shape),
            pl.BlockSpec((3, tm, LANES), _tab_index(tm, n_lat // tm, seq // tm)),
        ],
        out_specs=[qk_spec, qk_spec, pl.BlockSpec((C_HEADS // 2, tm, LANES), lambda i: (0, i, 0))],
        out_shape=[qk, qk, jax.ShapeDtypeStruct((C_HEADS // 2, t, LANES), BF16)],
        compiler_params=_cparams("parallel"),
        name="mix_projc",
    )(hx, w1, qg, kvg, wuq, wuk, wuv, tab)


def _pair_attn_kernel(*refs, n_seg, rs, diff, lam_init):
    if diff:
        lamv_ref, g_ref = refs[:2]
        refs = refs[2:]
        lv = lamv_ref[...]
        lam = (jnp.exp(jnp.sum(lv[0:1] * lv[1:2], axis=-1, keepdims=True))
               - jnp.exp(jnp.sum(lv[2:3] * lv[3:4], axis=-1, keepdims=True)) + lam_init)
    q_ref = refs[0]
    k_refs = refs[1:1 + n_seg]
    v_refs = refs[1 + n_seg:1 + 2 * n_seg]
    o_ref = refs[1 + 2 * n_seg]
    s_scr, p_scr = refs[-2:]
    chunks, col = [], 0
    for j, k in enumerate(k_refs):
        for lo in range(0, k.shape[-2], KEY_CHUNK):
            n = min(KEY_CHUNK, k.shape[-2] - lo)
            chunks.append((j, lo, n, col))
            col += n

    def lane_tiles(n):
        return range(0, n, LANES)

    def body(r, carry):
        rows = pl.ds(pl.multiple_of(r * rs, rs), rs)
        if diff:
            q = q_ref[rows, :]
            lane = lax.broadcasted_iota(jnp.int32, q.shape, 1)
            zero = jnp.zeros_like(q)
            qs = [jnp.where(lane < B_HEAD_DIM, q, zero), jnp.where(lane >= B_HEAD_DIM, q, zero)]
        else:
            qs = [q_ref[0, rows, :], q_ref[1, rows, :]]

        def scores(u, c, m_part):
            j, lo, n, off = c
            kk = k_refs[j][lo:lo + n, :] if diff else k_refs[j][u, lo:lo + n, :]
            s = _dot_nt(qs[u], kk)
            s_scr[u, :, off:off + n] = s
            for t in lane_tiles(n):
                m_part = s[:, t:t + LANES] if m_part is None else jnp.maximum(m_part, s[:, t:t + LANES])
            return m_part

        def softmax(u, c, m_b, l_part):
            j, lo, n, off = c
            for t in lane_tiles(n):
                e = jnp.exp2(s_scr[u, :, off + t:off + t + LANES] - m_b)
                p_scr[u, :, off + t:off + t + LANES] = e.astype(BF16)
                l_part = e if l_part is None else l_part + e
            return l_part

        def weighted(u, c, acc):
            j, lo, n, off = c
            t = _dot(p_scr[u, :, off:off + n], v_refs[j][lo:lo + n, :])
            return t if acc is None else acc + t

        def row_max(m_part):
            return jnp.broadcast_to(jnp.max(m_part, axis=-1, keepdims=True), m_part.shape)

        m0 = m1 = l0 = l1 = a0 = a1 = None
        for c in chunks:
            m0 = scores(0, c, m0)
        m0 = row_max(m0)
        for c in chunks:
            m1 = scores(1, c, m1)
            l0 = softmax(0, c, m0, l0)
        m1 = row_max(m1)
        for c in chunks:
            a0 = weighted(0, c, a0)
            l1 = softmax(1, c, m1, l1)
        for c in chunks:
            a1 = weighted(1, c, a1)
        l0 = jnp.sum(l0, axis=-1, keepdims=True)
        l1 = jnp.sum(l1, axis=-1, keepdims=True)
        if a0.shape[1] > LANES:
            first = pl.program_id(1) % 2 == 0
            a0 = jnp.where(first, a0[:, :LANES], a0[:, LANES:])
            a1 = jnp.where(first, a1[:, :LANES], a1[:, LANES:])
        if diff:
            o = a0 * (1.0 / l0) - a1 * (lam / l1)
            o_ref[rows, :] = (_rms(o, g_ref[...]) * (1.0 - lam_init)).astype(BF16)
        else:
            lane = lax.broadcasted_iota(jnp.int32, a0.shape, 1)
            o_ref[rows, :] = jnp.where(lane < C_VDIM, a0 / l0, a1 / l1).astype(BF16)
        return carry

    lax.fori_loop(0, o_ref.shape[0] // rs, body, 0)


def _drop_alias_arg(kern, pos, *refs):
    return kern(*(refs[:pos] + refs[pos + 1:]))


def _pair_attn(q, k, v, *, diff, n_batch, lq, q_row0, segs, lamv=None, g=None, lam_init=0.0, alias=None):
    t = q.shape[1]
    rs = _pick_tile((256, 128), lq)
    nk = sum(n for n, _ in segs)
    n_seg = len(segs)
    n_hblk = B_HEADS if diff else C_HEADS // 2
    lead = (None,) if diff else (2,)
    q_blk0 = q_row0 // lq
    seg_map = lambda o: (lambda b, h: (h, o + b, 0))
    in_specs = [pl.BlockSpec(lead + (lq, LANES), lambda b, h: (h, q_blk0 + b, 0))]
    in_specs += [pl.BlockSpec(lead + (n, LANES), seg_map(o)) for n, o in segs]
    vw = v.shape[-1]
    v_map = lambda o: (lambda b, h: (h // (vw // LANES), o + b, 0))
    in_specs += [pl.BlockSpec((None, n, vw), v_map(o)) for n, o in segs]
    args = [q] + [k] * n_seg + [v] * n_seg
    if diff:
        in_specs = [_resident(lamv.shape), _resident(g.shape)] + in_specs
        args = [lamv, g] + args
    kern = functools.partial(_pair_attn_kernel, n_seg=n_seg, rs=rs, diff=diff, lam_init=lam_init)
    io_alias = {}
    if alias is not None:
        kern = functools.partial(_drop_alias_arg, kern, len(args))
        io_alias = {len(args): 0}
        in_specs.append(pl.BlockSpec(memory_space=pl.ANY))
        args.append(alias)
    return pl.pallas_call(
        kern,
        grid=(n_batch, n_hblk),
        in_specs=in_specs,
        out_specs=pl.BlockSpec((lq, LANES), lambda b, h: (q_blk0 + b, h)),
        out_shape=jax.ShapeDtypeStruct((t, n_hblk * LANES), BF16),
        scratch_shapes=[pltpu.VMEM((2, rs, nk), F32), pltpu.VMEM((2, rs, nk), BF16)],
        input_output_aliases=io_alias,
        compiler_params=_cparams("parallel", "parallel"),
        name="attn_diff" if diff else "attn_mla",
    )(*args)


def _merge_kernel(x_ref, mod_ref, h_ref, oa_ref, ob_ref, oc_ref, wg_ref, bg_ref, wb_ref, wo_ref, o_ref):
    d = x_ref.shape[1]
    g = jax.nn.sigmoid(_dot(h_ref[...], wg_ref[...]) + bg_ref[...])
    y = (g[:, :d] * _dot(oa_ref[...], wb_ref[0]) + g[:, d:2 * d] * _dot(ob_ref[...], wb_ref[1])
         + g[:, 2 * d:] * _dot(oc_ref[...], wb_ref[2]))
    o_ref[...] = x_ref[...] + mod_ref[5:6, :] * _dot(y.astype(BF16), wo_ref[...])


def _merge(x, mod, hx, oa, ob, oc, wg, bg, wb, wo, *, rows, seq, n_batch):
    d = x.shape[1]
    tm = _pick_tile((512, 256, 128), rows, seq)
    row_spec = pl.BlockSpec((tm, d), lambda i: (i, 0))
    return pl.pallas_call(
        _merge_kernel,
        grid=(rows // tm,),
        in_specs=[
            row_spec,
            pl.BlockSpec((None, N_MOD, d), lambda i: (jnp.minimum(i * tm // seq, n_batch), 0, 0)),
            row_spec, row_spec, row_spec, row_spec,
            _resident(wg.shape), _resident(bg.shape), _resident(wb.shape), _resident(wo.shape),
        ],
        out_specs=row_spec,
        out_shape=jax.ShapeDtypeStruct((rows, d), F32),
        compiler_params=_cparams("parallel"),
        name="mix_merge",
    )(x, mod, hx, oa, ob, oc, wg, bg, wb, wo)


def kernel(x, c, ctx, c_ctx, ada_w, ada_b, norm_ffn1, ffn1_w13, ffn1_w2, norm_mix, w_in, b_gate, ln_v_g, ln_v_b, spatial_w, spatial_b, lambda_q1, lambda_k1, lambda_q2, lambda_k2, subln_g, q_norm_g, w_uq, kv_norm_g, w_ukv, w_branch, w_out, norm_ffn2, ffn2_w13, ffn2_w2, norm_final):
    n_batch, seq, d = x.shape
    n_ctx = ctx.shape[1]
    depth = ada_w.shape[0]
    dff = ffn1_w2.shape[1]
    n_lat = n_batch * seq
    n_all = n_lat + n_batch * n_ctx
    aw = ln_v_g.shape[1]
    bw = B_HEADS * 2 * B_HEAD_DIM

    n_cond = -(-(n_batch + 1) // 8) * 8
    cond = jnp.zeros((n_cond, d), F32).at[:n_batch].set(c).at[n_batch].set(c_ctx)
    mod_all = _ada_table(cond, ada_w, ada_b)[:, :n_batch + 1].reshape(depth, n_batch + 1, N_MOD, d)

    xs = jnp.concatenate([x.reshape(n_lat, d), ctx.reshape(n_batch * n_ctx, d)], axis=0)

    tp = _pick_tile((512, 256, 128), seq, n_batch * n_ctx)
    tab_b = _rope_tables(seq, tp, B_HEAD_DIM, 0, B_HEAD_DIM)
    tab_c = _rope_tables(seq, tp, C_ROPE, C_NOPE, LANES)

    o_a, o_q, o_k, o_v, o_cq, o_ckv, o_kr = itertools.accumulate(
        [2 * aw, bw, bw, bw, C_Q_RANK, C_KV_RANK, C_ROPE])
    row = lambda v: v.reshape(1, -1)
    lat_seg = (seq, 0)
    ctx_seg = (n_ctx, n_lat // n_ctx)

    for l in range(depth):
        last = l == depth - 1
        rows = n_lat if last else n_all
        mod = mod_all[l]
        lam_init = 0.8 - 0.6 * math.exp(-0.3 * l)

        wi = w_in[l]
        w_a = wi[:, :o_a].astype(BF16)
        w_qkv = wi[:, o_a:o_v].astype(BF16)
        w_kr = jnp.zeros((d, LANES), F32).at[:, C_NOPE:C_NOPE + C_ROPE].set(wi[:, o_ckv:o_kr])
        w_c1 = jnp.concatenate([wi[:, o_v:o_ckv], w_kr], axis=1).astype(BF16)
        w_g = wi[:, o_kr:].astype(BF16)
        wuq = jnp.pad(w_uq[l].reshape(C_Q_RANK, C_HEADS, C_NOPE + C_ROPE),
                      ((0, 0), (0, 0), (0, LANES - C_NOPE - C_ROPE))).reshape(C_Q_RANK, C_HEADS * LANES).astype(BF16)
        wukv = w_ukv[l].reshape(C_KV_RANK, C_HEADS, C_NOPE + C_VDIM)
        wuk = jnp.pad(wukv[:, :, :C_NOPE], ((0, 0), (0, 0), (0, LANES - C_NOPE))).reshape(
            C_KV_RANK, C_HEADS * LANES).astype(BF16)
        wuv = wukv[:, :, C_NOPE:].reshape(C_KV_RANK, C_HEADS * C_VDIM).astype(BF16)
        bs_full = jnp.repeat(spatial_b[l].T, aw // A_GROUPS, axis=1)
        lamv = jnp.stack([lambda_q1[l], lambda_k1[l], lambda_q2[l], lambda_k2[l]])

        xs, hx = _ffn(xs, mod, row(norm_ffn1[l]), ffn1_w13[l][:, :dff].astype(BF16),
                      ffn1_w13[l][:, dff:].astype(BF16), ffn1_w2[l].astype(BF16), row(norm_mix[l]),
                      k0=0, post="mix", rows=n_all, seq=seq, n_batch=n_batch)

        oa = _gmlp(hx, w_a, row(ln_v_g[l]), row(ln_v_b[l]), spatial_w[l].astype(BF16), bs_full, rows=rows)
        qb, kb, vb = _projb(hx, w_qkv, tab_b, tm=tp, n_lat=n_lat, seq=seq)
        qc, kc, vc = _projc(hx, w_c1, row(q_norm_g[l]), row(kv_norm_g[l]), wuq, wuk, wuv, tab_c,
                            tm=tp, n_lat=n_lat, seq=seq)

        g_sub = row(subln_g[l])
        diff_kw = dict(diff=True, n_batch=n_batch, lamv=lamv, g=g_sub, lam_init=lam_init)
        mla_kw = dict(diff=False, n_batch=n_batch)
        ob = _pair_attn(qb, kb, vb, lq=seq, q_row0=0, segs=[lat_seg, ctx_seg], **diff_kw)
        oc = _pair_attn(qc, kc, vc, lq=seq, q_row0=0, segs=[lat_seg, ctx_seg], **mla_kw)
        if not last:
            ob = _pair_attn(qb, kb, vb, lq=n_ctx, q_row0=n_lat, segs=[ctx_seg], alias=ob, **diff_kw)
            oc = _pair_attn(qc, kc, vc, lq=n_ctx, q_row0=n_lat, segs=[ctx_seg], alias=oc, **mla_kw)

        xs = _merge(xs, mod, hx, oa, ob, oc, w_g, row(b_gate[l]), w_branch[l].astype(BF16),
                    w_out[l].astype(BF16), rows=rows, seq=seq, n_batch=n_batch)

        xs = _ffn(xs, mod, row(norm_ffn2[l]), ffn2_w13[l][:, :dff].astype(BF16),
                  ffn2_w13[l][:, dff:].astype(BF16), ffn2_w2[l].astype(BF16), row(norm_final),
                  k0=6, post="final" if last else None, rows=rows, seq=seq, n_batch=n_batch)

    return xs[:n_lat].reshape(n_batch, seq, d)
```
